```python
import jax
import jax.numpy as jnp
from jax import lax
import numpy as np

D_MODEL = 1024
BATCH = 1
SEQ = 16384
DEPTH = 2

GRID_W = 64
CTX_LEN = 256
N_MOD = 9
D_FF = 2816
NA_HEADS = 8
NA_HEAD_DIM = 64
NA_WIDTH = NA_HEADS * NA_HEAD_DIM
NA_KH = 8
NA_KW = 16
POOL_WINDOWS = (2, 4, 8, 16)
POOL_GROUPS = len(POOL_WINDOWS)
POOL_WIDTH = D_MODEL - NA_WIDTH
POOL_GROUP_DIM = POOL_WIDTH // POOL_GROUPS
EVEN_IN_WIDTH = 3 * NA_WIDTH + POOL_WIDTH
EVEN_MIX_WIDTH = NA_WIDTH + POOL_WIDTH
CONV_WIDTH = D_MODEL
CONV_K = 3
N_EVEN = (DEPTH + 1) // 2
N_ODD = DEPTH // 2
RMS_EPS = 1e-6
NEG_INF = -1e30

kernel_name = "hybrid_natten_pool_shortconv_dit"


def rms_norm(x, g):
    x32 = x.astype(jnp.float32)
    y = x32 * lax.rsqrt(jnp.mean(x32 * x32, axis=-1, keepdims=True) + RMS_EPS)
    return (y * g.astype(jnp.float32)).astype(x.dtype)


def modulate(h, shift, scale):
    return h * (1 + scale) + shift


def adaln(cond, w, b):
    m = jax.nn.silu(cond) @ w + b
    return m.reshape(m.shape[:-1] + (N_MOD, D_MODEL))


def swiglu(h, w13, w2):
    a, b = jnp.split(h @ w13, 2, axis=-1)
    return (jax.nn.silu(a) * b) @ w2


def ffn_sublayer(h, m, g, w13, w2, base):
    hn = modulate(rms_norm(h, g), m[:, :, base], m[:, :, base + 1])
    return h + 0.5 * m[:, :, base + 2] * swiglu(hn, w13, w2)


def split_heads(t):
    return t.reshape(t.shape[:2] + (NA_HEADS, NA_HEAD_DIM))


def neighbourhood_attention(q, k, v, k_ctx, v_ctx, rpb):
    B, L, H, dh = q.shape
    R = L // GRID_W
    kh = min(NA_KH, R)
    kw = NA_KW
    scale = dh ** -0.5
    qg = q.reshape(B, R, GRID_W, H, dh)
    kg = k.reshape(B, R, GRID_W, H, dh)
    vg = v.reshape(B, R, GRID_W, H, dh)
    r = jnp.arange(R)
    row_start = jnp.clip(r - kh // 2, 0, R - kh)
    ridx = row_start[:, None] + jnp.arange(kh)[None, :]
    k_blk = kg[:, ridx]
    v_blk = vg[:, ridx]
    col = jnp.arange(GRID_W)
    col_start = jnp.clip(col - kw // 2, 0, GRID_W - kw)
    col_ok = (col[None, :] >= col_start[:, None]) & (col[None, :] < col_start[:, None] + kw)
    ri = ridx - r[:, None] + (NA_KH - 1)
    ci = jnp.clip(col[None, :] - col[:, None] + (NA_KW - 1), 0, 2 * NA_KW - 2)
    bias = rpb[:, ri[:, None, :, None], ci[None, :, None, :]].astype(jnp.float32)
    s_win = jnp.einsum('brqhd,brkwhd->bhrqkw', qg, k_blk).astype(jnp.float32) * scale + bias
    s_win = jnp.where(col_ok[:, None, :], s_win, NEG_INF)
    n_win = kh * GRID_W
    s_win = s_win.reshape(B, H, R, GRID_W, n_win)
    s_ctx = jnp.einsum('brqhd,bchd->bhrqc', qg, k_ctx).astype(jnp.float32) * scale
    p = jax.nn.softmax(jnp.concatenate([s_win, s_ctx], axis=-1), axis=-1).astype(v.dtype)
    p_win = p[..., :n_win].reshape(B, H, R, GRID_W, kh, GRID_W)
    o = (jnp.einsum('bhrqkw,brkwhd->brqhd', p_win, v_blk)
         + jnp.einsum('bhrqc,bchd->brqhd', p[..., n_win:], v_ctx))
    return o.reshape(B, L, H * dh)


def context_attention(q, k, v):
    B, C, H, dh = q.shape
    s = jnp.einsum('bqhd,bkhd->bhqk', q, k).astype(jnp.float32) * dh ** -0.5
    p = jax.nn.softmax(s, axis=-1).astype(v.dtype)
    return jnp.einsum('bhqk,bkhd->bqhd', p, v).reshape(B, C, H * dh)


def multiscale_pool(u, pool_w, pool_scale):
    B, L, _ = u.shape
    t = jnp.arange(L)
    ug = u.reshape(B, L, POOL_GROUPS, POOL_GROUP_DIM)
    outs = []
    for g, w in enumerate(POOL_WINDOWS):
        xg = ug[:, :, g].astype(jnp.float32)
        cs = jnp.pad(jnp.cumsum(xg, axis=1), ((0, 0), (1, 0), (0, 0)))
        lo = jnp.clip(t - w // 2, 0, L)
        hi = jnp.clip(t - w // 2 + w, 0, L)
        cnt = (hi - lo).astype(jnp.float32)[None, :, None]
        mean = (jnp.take(cs, hi, axis=1) - jnp.take(cs, lo, axis=1)) / cnt
        outs.append((mean - xg).astype(u.dtype) @ pool_w[g])
    return jnp.concatenate(outs, axis=-1) * pool_scale


def even_mixer(hl, hc, w_in, w_out, rpb, pool_w, pool_scale, ctx_out):
    q, k, v, u = jnp.split(hl @ w_in, [NA_WIDTH, 2 * NA_WIDTH, 3 * NA_WIDTH], axis=-1)
    k_c, v_c = jnp.split(hc @ w_in[:, NA_WIDTH:3 * NA_WIDTH], 2, axis=-1)
    k_c, v_c = split_heads(k_c), split_heads(v_c)
    att = neighbourhood_attention(split_heads(q), split_heads(k), split_heads(v), k_c, v_c, rpb)
    pool = multiscale_pool(u, pool_w, pool_scale)
    y_lat = jnp.concatenate([att, pool], axis=-1) @ w_out
    y_ctx = None
    if ctx_out:
        q_c = split_heads(hc @ w_in[:, :NA_WIDTH])
        u_c = hc @ w_in[:, 3 * NA_WIDTH:]
        att_c = context_attention(q_c, k_c, v_c)
        pool_c = multiscale_pool(u_c, pool_w, pool_scale)
        y_ctx = jnp.concatenate([att_c, pool_c], axis=-1) @ w_out
    return y_lat, y_ctx


def short_conv_mixer(h, w_in, conv_w, w_out):
    bg, cg, xin = jnp.split(h @ w_in, 3, axis=-1)
    z = cg * xin
    L = h.shape[1]
    zp = jnp.pad(z, ((0, 0), (1, 1), (0, 0)))
    y = zp[:, 0:L] * conv_w[0] + zp[:, 1:L + 1] * conv_w[1] + zp[:, 2:L + 2] * conv_w[2]
    return (bg * y) @ w_out


def setup_inputs(seed: int = 0) -> dict:
    key = jax.random.key(seed)
    ks = jax.random.split(key, 18)
    D = D_MODEL

    def nrm(k, shape, s):
        return jax.random.normal(k, shape, jnp.float32) * s

    return {
        "x": nrm(ks[0], (BATCH, SEQ, D), 1.0),
        "c": nrm(ks[1], (BATCH, D), 1.0),
        "ctx": nrm(ks[2], (BATCH, CTX_LEN, D), 1.0),
        "c_ctx": nrm(ks[3], (D,), 1.0),
        "mod_w": nrm(ks[4], (DEPTH, D, N_MOD * D), 0.5 * D ** -0.5),
        "mod_b": nrm(ks[5], (DEPTH, N_MOD * D), 0.01),
        "norm_g": 1.0 + nrm(ks[6], (DEPTH, 3, D), 0.02),
        "ffn_w13": nrm(ks[7], (DEPTH, 2, D, 2 * D_FF), D ** -0.5),
        "ffn_w2": nrm(ks[8], (DEPTH, 2, D_FF, D), D_FF ** -0.5),
        "even_w_in": nrm(ks[9], (N_EVEN, D, EVEN_IN_WIDTH), D ** -0.5),
        "even_w_out": nrm(ks[10], (N_EVEN, EVEN_MIX_WIDTH, D), EVEN_MIX_WIDTH ** -0.5),
        "na_rpb": nrm(ks[11], (N_EVEN, NA_HEADS, 2 * NA_KH - 1, 2 * NA_KW - 1), 0.1),
        "pool_w": nrm(ks[12], (N_EVEN, POOL_GROUPS, POOL_GROUP_DIM, POOL_GROUP_DIM), POOL_GROUP_DIM ** -0.5),
        "pool_scale": 1.0 + nrm(ks[13], (N_EVEN, POOL_WIDTH), 0.1),
        "conv_w_in": nrm(ks[14], (N_ODD, D, 3 * CONV_WIDTH), D ** -0.5),
        "conv_w": nrm(ks[15], (N_ODD, CONV_K, CONV_WIDTH), CONV_K ** -0.5),
        "conv_w_out": nrm(ks[16], (N_ODD, CONV_WIDTH, D), CONV_WIDTH ** -0.5),
        "final_g": 1.0 + nrm(ks[17], (D,), 0.02),
    }


def reference(x, c, ctx, c_ctx, mod_w, mod_b, norm_g, ffn_w13, ffn_w2, even_w_in, even_w_out,
              na_rpb, pool_w, pool_scale, conv_w_in, conv_w, conv_w_out, final_g):
    for i in range(DEPTH):
        even = (i % 2 == 0)
        ctx_out = any(j % 2 == 0 for j in range(i + 1, DEPTH))
        ctx_here = even or ctx_out
        m_l = adaln(c[:, None, :], mod_w[i], mod_b[i])
        m_c = adaln(c_ctx[None, None, :], mod_w[i], mod_b[i])

        x = ffn_sublayer(x, m_l, norm_g[i, 0], ffn_w13[i, 0], ffn_w2[i, 0], 0)
        if ctx_here:
            ctx = ffn_sublayer(ctx, m_c, norm_g[i, 0], ffn_w13[i, 0], ffn_w2[i, 0], 0)

        xn = modulate(rms_norm(x, norm_g[i, 1]), m_l[:, :, 3], m_l[:, :, 4])
        y_c = None
        if even:
            e = i // 2
            cn = modulate(rms_norm(ctx, norm_g[i, 1]), m_c[:, :, 3], m_c[:, :, 4])
            y_l, y_c = even_mixer(xn, cn, even_w_in[e], even_w_out[e], na_rpb[e],
                                  pool_w[e], pool_scale[e], ctx_out)
        else:
            o = i // 2
            y_l = short_conv_mixer(xn, conv_w_in[o], conv_w[o], conv_w_out[o])
            if ctx_out:
                cn = modulate(rms_norm(ctx, norm_g[i, 1]), m_c[:, :, 3], m_c[:, :, 4])
                y_c = short_conv_mixer(cn, conv_w_in[o], conv_w[o], conv_w_out[o])
        x = x + m_l[:, :, 5] * y_l

        x = ffn_sublayer(x, m_l, norm_g[i, 2], ffn_w13[i, 1], ffn_w2[i, 1], 6)
        if ctx_out:
            ctx = ctx + m_c[:, :, 5] * y_c
            ctx = ffn_sublayer(ctx, m_c, norm_g[i, 2], ffn_w13[i, 1], ffn_w2[i, 1], 6)

    return rms_norm(x, final_g)
```

```python
import functools

import jax
import jax.numpy as jnp
from jax import lax
from jax.experimental import pallas as pl
from jax.experimental.pallas import tpu as pltpu

F32 = jnp.float32
BF16 = jnp.bfloat16

GRID_W = 64
N_MOD = 9
NA_HEADS = 8
NA_HEAD_DIM = 64
NA_WIDTH = NA_HEADS * NA_HEAD_DIM
NA_KH = 8
NA_KW = 16
POOL_WINDOWS = (2, 4, 8, 16)
POOL_GROUP_DIM = 128
RMS_EPS = 1e-6
NEG_INF = -1e30

SUBLANES = 8
LANES = 128
MXU_DIM = 256
VMEM_LIMIT_BYTES = 56 * 1024 * 1024

ATT_ROWS = NA_KH // 2
ATT_TOKENS = ATT_ROWS * GRID_W
ATT_KEYS = 3 * ATT_TOKENS
HEADS_PER_GROUP = MXU_DIM // NA_HEAD_DIM
POOL_HALO = SUBLANES


def _resident(shape):
    zeros = (0,) * len(shape)
    return pl.BlockSpec(shape, lambda *_: zeros, pipeline_mode=pl.Buffered(1))


def _mod_spec(mod, layer, cond):
    return pl.BlockSpec((None, None) + mod.shape[2:], lambda *_: (layer, cond, 0, 0))


def _params(n_grid_dims):
    return pltpu.CompilerParams(
        dimension_semantics=("arbitrary",) * n_grid_dims,
        vmem_limit_bytes=VMEM_LIMIT_BYTES,
    )


def _rms_norm(x, g):
    return x * lax.rsqrt(jnp.mean(x * x, axis=-1, keepdims=True) + RMS_EPS) * g


def _norm_mod(x, g, shift, scale):
    return _rms_norm(x, g) * (1.0 + scale) + shift


def _dot(a, b):
    return jnp.dot(a, b, preferred_element_type=F32)


def _dot_nt(a, b):
    return lax.dot_general(a, b, (((1,), (1,)), ((), ())), preferred_element_type=F32)


def _adaln_kernel(cond_ref, w_ref, b_ref, o_ref):
    d, tn = w_ref.shape

    def body(kb, accs):
        a0, a1 = accs
        k0 = pl.multiple_of(kb * SUBLANES, SUBLANES)
        w = w_ref[pl.ds(k0, SUBLANES), :]
        cnd = cond_ref[pl.ds(k0, SUBLANES), :]
        s = cnd * jax.nn.sigmoid(cnd)
        return a0 + w * s[:, 0:1], a1 + w * s[:, 1:2]

    zero = jnp.zeros((SUBLANES, tn), F32)
    a0, a1 = lax.fori_loop(0, d // SUBLANES, body, (zero, zero), unroll=8)
    bias = b_ref[...]
    o_ref[0:1, :] = jnp.sum(a0, axis=0, keepdims=True) + bias
    o_ref[1:2, :] = jnp.sum(a1, axis=0, keepdims=True) + bias


def _adaln(cond_cols, mod_w, mod_b):
    depth, d, n = mod_w.shape
    tn = 1024
    out = pl.pallas_call(
        _adaln_kernel,
        grid=(depth, n // tn),
        in_specs=[
            pl.BlockSpec((d, 2), lambda l, j: (0, 0)),
            pl.BlockSpec((None, d, tn), lambda l, j: (l, 0, j)),
            pl.BlockSpec((None, 1, tn), lambda l, j: (l, 0, j)),
        ],
        out_specs=pl.BlockSpec((None, 2, tn), lambda l, j: (l, 0, j)),
        out_shape=jax.ShapeDtypeStruct((depth, 2, n), F32),
        compiler_params=_params(2),
        name="adaln",
    )(cond_cols, mod_w, mod_b.reshape(depth, 1, n))
    return out.reshape(depth, 2, N_MOD, d)


def _ffn_kernel(*refs, base, f_chunk, final_norm):
    if final_norm:
        x_ref, mod_ref, g_ref, w13_ref, w2_ref, fg_ref, o_ref, hn_ref, acc_ref = refs
    else:
        x_ref, mod_ref, g_ref, w13_ref, w2_ref, o_ref, hn_ref, acc_ref = refs
    d_ff = w2_ref.shape[0]
    hn_ref[...] = _norm_mod(
        x_ref[...], g_ref[...], mod_ref[base:base + 1, :], mod_ref[base + 1:base + 2, :]
    ).astype(BF16)
    for ci in range(d_ff // f_chunk):
        lo = ci * f_chunk
        hn = hn_ref[...]
        a = _dot(hn, w13_ref[:, lo:lo + f_chunk])
        b = _dot(hn, w13_ref[:, d_ff + lo:d_ff + lo + f_chunk])
        act = (a * jax.nn.sigmoid(a) * b).astype(BF16)
        part = _dot(act, w2_ref[lo:lo + f_chunk, :])
        if ci == 0:
            acc_ref[...] = part
        else:
            acc_ref[...] += part
    res = x_ref[...] + (0.5 * mod_ref[base + 2:base + 3, :]) * acc_ref[...]
    if final_norm:
        res = _rms_norm(res, fg_ref[...])
    o_ref[...] = res


def _ffn(x, mod, layer, cond, g, w13, w2, base, final_g=None):
    n_tok, d = x.shape
    d_ff = w2.shape[0]
    tm = min(n_tok, 512)
    f_chunk = MXU_DIM
    in_specs = [
        pl.BlockSpec((tm, d), lambda i: (i, 0)),
        _mod_spec(mod, layer, cond),
        _resident((1, d)),
        _resident((d, 2 * d_ff)),
        _resident((d_ff, d)),
    ]
    args = [x, mod, g, w13, w2]
    if final_g is not None:
        in_specs.append(_resident((1, d)))
        args.append(final_g)
    return pl.pallas_call(
        functools.partial(_ffn_kernel, base=base, f_chunk=f_chunk, final_norm=final_g is not None),
        grid=(n_tok // tm,),
        in_specs=in_specs,
        out_specs=pl.BlockSpec((tm, d), lambda i: (i, 0)),
        out_shape=jax.ShapeDtypeStruct((n_tok, d), F32),
        scratch_shapes=[pltpu.VMEM((tm, d), BF16), pltpu.VMEM((tm, d), F32)],
        compiler_params=_params(1),
        name="ffn_final" if final_g is not None else "ffn",
    )(*args)


def _even_in_kernel(x_ref, mod_ref, g_ref, w_ref, q_ref, k_ref, v_ref, u_ref):
    hn = _norm_mod(x_ref[...], g_ref[...], mod_ref[3:4, :], mod_ref[4:5, :]).astype(BF16)
    w = NA_WIDTH
    q_ref[...] = (_dot(hn, w_ref[:, 0:w]) * (NA_HEAD_DIM ** -0.5)).astype(BF16)
    k_ref[...] = _dot(hn, w_ref[:, w:2 * w]).astype(BF16)
    v_ref[...] = _dot(hn, w_ref[:, 2 * w:3 * w]).astype(BF16)
    u_ref[...] = _dot(hn, w_ref[:, 3 * w:])


def _even_in(x, mod, layer, cond, g, w_in):
    n_tok, d = x.shape
    n_out = w_in.shape[1]
    pool_width = n_out - 3 * NA_WIDTH
    tm = min(n_tok, 512)
    tok_spec = lambda width: pl.BlockSpec((tm, width), lambda i: (i, 0))
    return pl.pallas_call(
        _even_in_kernel,
        grid=(n_tok // tm,),
        in_specs=[tok_spec(d), _mod_spec(mod, layer, cond), _resident((1, d)), _resident((d, n_out))],
        out_specs=[tok_spec(NA_WIDTH), tok_spec(NA_WIDTH), tok_spec(NA_WIDTH), tok_spec(pool_width)],
        out_shape=[
            jax.ShapeDtypeStruct((n_tok, NA_WIDTH), BF16),
            jax.ShapeDtypeStruct((n_tok, NA_WIDTH), BF16),
            jax.ShapeDtypeStruct((n_tok, NA_WIDTH), BF16),
            jax.ShapeDtypeStruct((n_tok, pool_width), F32),
        ],
        compiler_params=_params(1),
        name="even_in",
    )(x, mod, g, w_in)


N_RI = 2 * NA_KH - 1
N_CI = 2 * NA_KW - 1
KEY_ROWS = 3 * ATT_ROWS


def _bias_table_kernel(rpb_ref, o_ref):
    head = pl.program_id(0)
    shape = (GRID_W, 2 * GRID_W)
    q = lax.broadcasted_iota(jnp.int32, shape, 0)
    lane = lax.broadcasted_iota(jnp.int32, shape, 1)
    second = lane >= GRID_W
    kc = jnp.where(second, lane - GRID_W, lane)
    diff = kc - q + (NA_KW - 1)
    col_start = jnp.clip(q - NA_KW // 2, 0, GRID_W - NA_KW)
    col_ok = (kc >= col_start) & (kc < col_start + NA_KW)
    neg = jnp.full(shape, NEG_INF, F32)

    def pair_tile(ri0):
        def body(ci, tile):
            base = head * (N_RI * N_CI) + ci
            v0 = rpb_ref[base + ri0 * N_CI]
            v1 = rpb_ref[base + (ri0 + 1) * N_CI]
            return jnp.where(diff == ci, jnp.where(second, v1, v0), tile)

        tile = lax.fori_loop(0, N_CI, body, jnp.zeros(shape, F32))
        return jnp.where(col_ok, tile, neg)

    tiles = {}
    for a in range(ATT_ROWS):
        for j in range(0, KEY_ROWS, 2):
            ri0 = j - a + ATT_ROWS - 1
            if ri0 not in tiles:
                tiles[ri0] = pair_tile(ri0)

    windows = (
        lambda a: (ATT_ROWS, ATT_ROWS + NA_KH),
        lambda a: (a, a + NA_KH),
        lambda a: (0, NA_KH),
    )
    for kind, window in enumerate(windows):
        for a in range(ATT_ROWS):
            lo, hi = window(a)
            for j in range(0, KEY_ROWS, 2):
                tile = tiles[j - a + ATT_ROWS - 1]
                ok0, ok1 = lo <= j < hi, lo <= j + 1 < hi
                if ok0 and ok1:
                    val = tile
                elif ok0:
                    val = jnp.where(second, neg, tile)
                elif ok1:
                    val = jnp.where(second, tile, neg)
                else:
                    val = neg
                o_ref[kind, a * GRID_W:(a + 1) * GRID_W, j * GRID_W:(j + 2) * GRID_W] = val


def _bias_tables(rpb):
    heads = rpb.shape[0]
    return pl.pallas_call(
        _bias_table_kernel,
        grid=(heads,),
        in_specs=[pl.BlockSpec(memory_space=pltpu.SMEM)],
        out_specs=pl.BlockSpec((3, None, ATT_TOKENS, ATT_KEYS), lambda h: (0, h, 0, 0)),
        out_shape=jax.ShapeDtypeStruct((3, heads, ATT_TOKENS, ATT_KEYS), F32),
        compiler_params=_params(1),
        name="bias_tables",
    )(rpb.reshape(-1))


def _even_mix_kernel(x_ref, q_ref, kp_ref, kc_ref, kn_ref, vp_ref, vc_ref, vn_ref, kx_ref, vx_ref,
                     tab_ref, uc_ref, up_ref, un_ref, pw_ref, ps_ref, wo_ref, mod_ref,
                     o_ref, uext_ref, mix_ref, *, seq_len):
    blk = pl.program_id(0)
    n_blk = pl.num_programs(0)
    tq = ATT_TOKENS

    lane = lax.broadcasted_iota(jnp.int32, (1, MXU_DIM), 1)
    for hg in range(NA_HEADS // HEADS_PER_GROUP):
        sl = slice(hg * MXU_DIM, (hg + 1) * MXU_DIM)
        q4 = q_ref[:, sl]
        keys = (kp_ref[:, sl], kc_ref[:, sl], kn_ref[:, sl], kx_ref[:, sl])
        vals = (vp_ref[:, sl], vc_ref[:, sl], vn_ref[:, sl], vx_ref[:, sl])
        out4 = jnp.zeros((tq, MXU_DIM), F32)
        for hh in range(HEADS_PER_GROUP):
            head = hg * HEADS_PER_GROUP + hh
            in_head = (lane >= hh * NA_HEAD_DIM) & (lane < (hh + 1) * NA_HEAD_DIM)
            qh = jnp.where(in_head, q4, jnp.zeros_like(q4))
            s = [_dot_nt(qh, kk) for kk in keys]
            for j in range(3):
                s[j] = s[j] + tab_ref[head, :, j * tq:(j + 1) * tq]
            m = jnp.max(jnp.maximum(jnp.maximum(s[0], s[1]), jnp.maximum(s[2], s[3])),
                        axis=1, keepdims=True)
            p = [jnp.exp(sj - m) for sj in s]
            denom = jnp.sum((p[0] + p[1]) + (p[2] + p[3]), axis=1, keepdims=True)
            pv = _dot(p[0].astype(BF16), vals[0])
            for j in range(1, 4):
                pv = pv + _dot(p[j].astype(BF16), vals[j])
            out4 = jnp.where(in_head, pv / denom, out4)
        mix_ref[:, sl] = out4.astype(BF16)

    halo = POOL_HALO
    uext_ref[0:halo, :] = jnp.where(blk > 0, up_ref[...], 0.0)
    uext_ref[halo:halo + tq, :] = uc_ref[...]
    uext_ref[halo + tq:2 * halo + tq, :] = jnp.where(blk < n_blk - 1, un_ref[...], 0.0)
    t = blk * tq + lax.broadcasted_iota(jnp.int32, (tq, 1), 0)
    for g, w in enumerate(POOL_WINDOWS):
        cs = slice(g * POOL_GROUP_DIM, (g + 1) * POOL_GROUP_DIM)
        first = halo - w // 2
        total = uext_ref[first:first + tq, cs]
        for off in range(1, w):
            total = total + uext_ref[first + off:first + off + tq, cs]
        lo = jnp.clip(t - w // 2, 0, seq_len)
        hi = jnp.clip(t - w // 2 + w, 0, seq_len)
        mean = total / (hi - lo).astype(F32)
        delta = (mean - uc_ref[:, cs]).astype(BF16)
        pooled = _dot(delta, pw_ref[g]) * ps_ref[:, cs]
        mix_ref[:, NA_WIDTH + g * POOL_GROUP_DIM:NA_WIDTH + (g + 1) * POOL_GROUP_DIM] = pooled.astype(BF16)

    y = _dot(mix_ref[...], wo_ref[...])
    o_ref[...] = x_ref[...] + mod_ref[5:6, :] * y


def _even_mix(x, q, k, v, k_ctx, v_ctx, tables, u, pool_w, pool_scale, w_out, mod, layer):
    n_tok, d = x.shape
    n_ctx = k_ctx.shape[0]
    pool_width = u.shape[1]
    tq = ATT_TOKENS
    n_blk = n_tok // tq
    halo_per_blk = tq // POOL_HALO
    n_halo = n_tok // POOL_HALO

    cur = lambda width: pl.BlockSpec((tq, width), lambda i: (i, 0))
    prev = pl.BlockSpec((tq, NA_WIDTH), lambda i: (jnp.maximum(i - 1, 0), 0))
    nxt = pl.BlockSpec((tq, NA_WIDTH), lambda i: (jnp.minimum(i + 1, n_blk - 1), 0))
    kind = lambda i: jnp.where(i == 0, 0, jnp.where(i == n_blk - 1, 2, 1))
    return pl.pallas_call(
        functools.partial(_even_mix_kernel, seq_len=n_tok),
        grid=(n_blk,),
        in_specs=[
            cur(d), cur(NA_WIDTH),
            prev, cur(NA_WIDTH), nxt,
            prev, cur(NA_WIDTH), nxt,
            _resident((n_ctx, NA_WIDTH)), _resident((n_ctx, NA_WIDTH)),
            pl.BlockSpec((None, NA_HEADS, tq, ATT_KEYS), lambda i: (kind(i), 0, 0, 0)),
            cur(pool_width),
            pl.BlockSpec((POOL_HALO, pool_width), lambda i: (jnp.maximum(i * halo_per_blk - 1, 0), 0)),
            pl.BlockSpec((POOL_HALO, pool_width), lambda i: (jnp.minimum((i + 1) * halo_per_blk, n_halo - 1), 0)),
            _resident(pool_w.shape), _resident((1, pool_width)), _resident(w_out.shape),
            _mod_spec(mod, layer, 0),
        ],
        out_specs=cur(d),
        out_shape=jax.ShapeDtypeStruct((n_tok, d), F32),
        scratch_shapes=[
            pltpu.VMEM((tq + 2 * POOL_HALO, pool_width), F32),
            pltpu.VMEM((tq, d), BF16),
        ],
        compiler_params=_params(1),
        name="even_mix",
    )(x, q, k, k, k, v, v, v, k_ctx, v_ctx, tables, u, u, u, pool_w, pool_scale, w_out, mod)


def _conv_mix_kernel(xc_ref, xp_ref, xn_ref, mod_ref, g_ref, win_ref, cw_ref, wo_ref, o_ref, z_ref):
    blk = pl.program_id(0)
    n_blk = pl.num_programs(0)
    tm, d = xc_ref.shape
    halo = SUBLANES
    g, shift, scale = g_ref[...], mod_ref[3:4, :], mod_ref[4:5, :]

    def gate_input(hn):
        cx = _dot(hn, win_ref[:, d:])
        return cx[:, :d] * cx[:, d:]

    hn = _norm_mod(xc_ref[...], g, shift, scale).astype(BF16)
    bg = _dot(hn, win_ref[:, :d])
    z_ref[halo:halo + tm, :] = gate_input(hn)
    x_halo = jnp.concatenate([xp_ref[...], xn_ref[...]], axis=0)
    z_halo = gate_input(_norm_mod(x_halo, g, shift, scale).astype(BF16))
    z_ref[0:halo, :] = jnp.where(blk > 0, z_halo[:halo], 0.0)
    z_ref[halo + tm:, :] = jnp.where(blk < n_blk - 1, z_halo[halo:], 0.0)

    y = (z_ref[halo - 1:halo - 1 + tm, :] * cw_ref[0:1, :]
         + z_ref[halo:halo + tm, :] * cw_ref[1:2, :]
         + z_ref[halo + 1:halo + 1 + tm, :] * cw_ref[2:3, :])
    out = _dot((bg * y).astype(BF16), wo_ref[...])
    o_ref[...] = xc_ref[...] + mod_ref[5:6, :] * out


def _conv_mix(x, mod, layer, g, w_in, conv_w, w_out):
    n_tok, d = x.shape
    tm = min(n_tok, 512)
    n_blk = n_tok // tm
    halo_per_blk = tm // SUBLANES
    n_halo = n_tok // SUBLANES
    return pl.pallas_call(
        _conv_mix_kernel,
        grid=(n_blk,),
        in_specs=[
            pl.BlockSpec((tm, d), lambda i: (i, 0)),
            pl.BlockSpec((SUBLANES, d), lambda i: (jnp.maximum(i * halo_per_blk - 1, 0), 0)),
            pl.BlockSpec((SUBLANES, d), lambda i: (jnp.minimum((i + 1) * halo_per_blk, n_halo - 1), 0)),
            _mod_spec(mod, layer, 0),
            _resident((1, d)), _resident(w_in.shape), _resident(conv_w.shape), _resident(w_out.shape),
        ],
        out_specs=pl.BlockSpec((tm, d), lambda i: (i, 0)),
        out_shape=jax.ShapeDtypeStruct((n_tok, d), F32),
        scratch_shapes=[pltpu.VMEM((tm + 2 * SUBLANES, d), F32)],
        compiler_params=_params(1),
        name="conv_mix",
    )(x, x, x, mod, g, w_in, conv_w, w_out)


def kernel(x, c, ctx, c_ctx, mod_w, mod_b, norm_g, ffn_w13, ffn_w2, even_w_in, even_w_out,
           na_rpb, pool_w, pool_scale, conv_w_in, conv_w, conv_w_out, final_g):
    batch, seq_len, d = x.shape
    depth = mod_w.shape[0]
    assert batch == 1 and c.shape[0] == 1 and ctx.shape[0] == 1
    assert seq_len % ATT_TOKENS == 0 and seq_len // GRID_W >= NA_KH and GRID_W >= NA_KW
    assert even_w_in.shape[2] == 3 * NA_WIDTH + len(POOL_WINDOWS) * POOL_GROUP_DIM

    xs = x.reshape(seq_len, d)
    cs = ctx.reshape(ctx.shape[1], d)
    cond_cols = jnp.stack([c[0], c_ctx], axis=1)
    mod = _adaln(cond_cols, mod_w, mod_b)

    w13 = ffn_w13.astype(BF16)
    w2 = ffn_w2.astype(BF16)
    e_in = even_w_in.astype(BF16)
    e_out = even_w_out.astype(BF16)
    p_w = pool_w.astype(BF16)
    c_in = conv_w_in.astype(BF16)
    c_out = conv_w_out.astype(BF16)
    gains = norm_g.reshape(depth, 3, 1, d)
    fin = final_g.reshape(1, d)

    for i in range(depth):
        even = i % 2 == 0
        if any(j % 2 == 0 for j in range(i + 1, depth)):
            raise NotImplementedError("context stream carried past layer %d" % i)
        last = i == depth - 1
        xs = _ffn(xs, mod, i, 0, gains[i, 0], w13[i, 0], w2[i, 0], 0)
        if even:
            e = i // 2
            cs = _ffn(cs, mod, i, 1, gains[i, 0], w13[i, 0], w2[i, 0], 0)
            q, k, v, u = _even_in(xs, mod, i, 0, gains[i, 1], e_in[e])
            _, k_ctx, v_ctx, _ = _even_in(cs, mod, i, 1, gains[i, 1], e_in[e])
            tables = _bias_tables(na_rpb[e])
            xs = _even_mix(xs, q, k, v, k_ctx, v_ctx, tables, u, p_w[e],
                           pool_scale[e].reshape(1, -1), e_out[e], mod, i)
        else:
            o = i // 2
            xs = _conv_mix(xs, mod, i, gains[i, 1], c_in[o], conv_w[o], c_out[o])
        xs = _ffn(xs, mod, i, 0, gains[i, 2], w13[i, 1], w2[i, 1], 6, final_g=fin if last else None)
    return xs.reshape(batch, seq_len, d)
```

```python
import functools

import jax
import jax.numpy as jnp
from jax import lax
from jax.experimental import pallas as pl
from jax.experimental.pallas import tpu as pltpu

F32 = jnp.float32
BF16 = jnp.bfloat16

GRID_W = 64
N_MOD = 9
NA_HEADS = 8
NA_HEAD_DIM = 64
NA_WIDTH = NA_HEADS * NA_HEAD_DIM
NA_KH = 8
NA_KW = 16
POOL_WINDOWS = (2, 4, 8, 16)
POOL_GROUP_DIM = 128
RMS_EPS = 1e-6
NEG_INF = -1e30

SUBLANES = 8
LANES = 128
MXU_DIM = 256
VMEM_LIMIT_BYTES = 56 * 1024 * 1024

ATT_ROWS = NA_KH // 2
ATT_TOKENS = ATT_ROWS * GRID_W
ATT_KEYS = 3 * ATT_TOKENS
HEADS_PER_GROUP = MXU_DIM // NA_HEAD_DIM
POOL_HALO = SUBLANES


def _resident(shape):
    zeros = (0,) * len(shape)
    return pl.BlockSpec(shape, lambda *_: zeros, pipeline_mode=pl.Buffered(1))


def _resident_at(arr, *lead):
    zeros = (0,) * (arr.ndim - len(lead))
    return pl.BlockSpec((None,) * len(lead) + arr.shape[len(lead):], lambda *_: tuple(lead) + zeros,
                        pipeline_mode=pl.Buffered(1))


def _mod_spec(mod, layer, cond):
    return pl.BlockSpec((None, None) + mod.shape[2:], lambda *_: (layer, cond, 0, 0))


def _params(n_grid_dims):
    return pltpu.CompilerParams(
        dimension_semantics=("arbitrary",) * n_grid_dims,
        vmem_limit_bytes=VMEM_LIMIT_BYTES,
    )


def _rms_norm(x, g):
    return x * lax.rsqrt(jnp.mean(x * x, axis=-1, keepdims=True) + RMS_EPS) * g


def _norm_mod(x, g, shift, scale):
    return _rms_norm(x, g) * (1.0 + scale) + shift


def _dot(a, b):
    return jnp.dot(a, b, preferred_element_type=F32)


def _dot_nt(a, b):
    return lax.dot_general(a, b, (((1,), (1,)), ((), ())), preferred_element_type=F32)


def _adaln_kernel(cond_ref, w_ref, b_ref, o_ref):
    d, tn = w_ref.shape

    def body(kb, accs):
        a0, a1 = accs
        k0 = pl.multiple_of(kb * SUBLANES, SUBLANES)
        w = w_ref[pl.ds(k0, SUBLANES), :]
        cnd = cond_ref[pl.ds(k0, SUBLANES), :]
        s = cnd * jax.nn.sigmoid(cnd)
        return a0 + w * s[:, 0:1], a1 + w * s[:, 1:2]

    zero = jnp.zeros((SUBLANES, tn), F32)
    a0, a1 = lax.fori_loop(0, d // SUBLANES, body, (zero, zero), unroll=8)
    bias = b_ref[...]
    o_ref[0:1, :] = jnp.sum(a0, axis=0, keepdims=True) + bias
    o_ref[1:2, :] = jnp.sum(a1, axis=0, keepdims=True) + bias


def _adaln(cond_cols, mod_w, mod_b):
    depth, d, n = mod_w.shape
    tn = 1024
    out = pl.pallas_call(
        _adaln_kernel,
        grid=(depth, n // tn),
        in_specs=[
            pl.BlockSpec((d, 2), lambda l, j: (0, 0)),
            pl.BlockSpec((None, d, tn), lambda l, j: (l, 0, j)),
            pl.BlockSpec((None, 1, tn), lambda l, j: (l, 0, j)),
        ],
        out_specs=pl.BlockSpec((None, 2, tn), lambda l, j: (l, 0, j)),
        out_shape=jax.ShapeDtypeStruct((depth, 2, n), F32),
        compiler_params=_params(2),
        name="adaln",
    )(cond_cols, mod_w, mod_b.reshape(depth, 1, n))
    return out.reshape(depth, 2, N_MOD, d)


def _ffn_kernel(*refs, base, f_chunk, final_norm):
    if final_norm:
        x_ref, mod_ref, g_ref, w13_ref, w2_ref, fg_ref, o_ref, hn_ref, acc_ref = refs
    else:
        x_ref, mod_ref, g_ref, w13_ref, w2_ref, o_ref, hn_ref, acc_ref = refs
    d_ff = w2_ref.shape[0]
    hn_ref[...] = _norm_mod(
        x_ref[...], g_ref[...], mod_ref[base:base + 1, :], mod_ref[base + 1:base + 2, :]
    ).astype(BF16)
    for ci in range(d_ff // f_chunk):
        lo = ci * f_chunk
        hn = hn_ref[...]
        a = _dot(hn, w13_ref[:, lo:lo + f_chunk])
        b = _dot(hn, w13_ref[:, d_ff + lo:d_ff + lo + f_chunk])
        act = (a * jax.nn.sigmoid(a) * b).astype(BF16)
        part = _dot(act, w2_ref[lo:lo + f_chunk, :])
        if ci == 0:
            acc_ref[...] = part
        else:
            acc_ref[...] += part
    res = x_ref[...] + (0.5 * mod_ref[base + 2:base + 3, :]) * acc_ref[...]
    if final_norm:
        res = _rms_norm(res, fg_ref[...])
    o_ref[...] = res


def _ffn(x, mod, layer, cond, gains, w13, w2, which, final_g=None):
    n_tok, d = x.shape
    tm = min(n_tok, 512)
    f_chunk = MXU_DIM
    base = 6 * which
    in_specs = [
        pl.BlockSpec((tm, d), lambda i: (i, 0)),
        _mod_spec(mod, layer, cond),
        _resident_at(gains, layer, 2 * which),
        _resident_at(w13, layer, which),
        _resident_at(w2, layer, which),
    ]
    args = [x, mod, gains, w13, w2]
    if final_g is not None:
        in_specs.append(_resident((1, d)))
        args.append(final_g)
    return pl.pallas_call(
        functools.partial(_ffn_kernel, base=base, f_chunk=f_chunk, final_norm=final_g is not None),
        grid=(n_tok // tm,),
        in_specs=in_specs,
        out_specs=pl.BlockSpec((tm, d), lambda i: (i, 0)),
        out_shape=jax.ShapeDtypeStruct((n_tok, d), F32),
        scratch_shapes=[pltpu.VMEM((tm, d), BF16), pltpu.VMEM((tm, d), F32)],
        compiler_params=_params(1),
        name="ffn_final" if final_g is not None else "ffn",
    )(*args)


def _even_in_kernel(x_ref, mod_ref, g_ref, w_ref, q_ref, k_ref, v_ref, u_ref):
    hn = _norm_mod(x_ref[...], g_ref[...], mod_ref[3:4, :], mod_ref[4:5, :]).astype(BF16)
    w = NA_WIDTH
    q_ref[...] = (_dot(hn, w_ref[:, 0:w]) * (NA_HEAD_DIM ** -0.5)).astype(BF16)
    k_ref[...] = _dot(hn, w_ref[:, w:2 * w]).astype(BF16)
    v_ref[...] = _dot(hn, w_ref[:, 2 * w:3 * w]).astype(BF16)
    u_ref[...] = _dot(hn, w_ref[:, 3 * w:])


def _even_in(x, mod, layer, cond, gains, w_in, e):
    n_tok, d = x.shape
    n_out = w_in.shape[2]
    pool_width = n_out - 3 * NA_WIDTH
    tm = min(n_tok, 512)
    tok_spec = lambda width: pl.BlockSpec((tm, width), lambda i: (i, 0))
    return pl.pallas_call(
        _even_in_kernel,
        grid=(n_tok // tm,),
        in_specs=[tok_spec(d), _mod_spec(mod, layer, cond), _resident_at(gains, layer, 1),
                  _resident_at(w_in, e)],
        out_specs=[tok_spec(NA_WIDTH), tok_spec(NA_WIDTH), tok_spec(NA_WIDTH), tok_spec(pool_width)],
        out_shape=[
            jax.ShapeDtypeStruct((n_tok, NA_WIDTH), BF16),
            jax.ShapeDtypeStruct((n_tok, NA_WIDTH), BF16),
            jax.ShapeDtypeStruct((n_tok, NA_WIDTH), BF16),
            jax.ShapeDtypeStruct((n_tok, pool_width), F32),
        ],
        compiler_params=_params(1),
        name="even_in",
    )(x, mod, gains, w_in)


N_RI = 2 * NA_KH - 1
N_CI = 2 * NA_KW - 1
KEY_ROWS = 3 * ATT_ROWS


def _bias_table_kernel(rpb_ref, o_ref):
    shape = (GRID_W, 2 * GRID_W)
    q = lax.broadcasted_iota(jnp.int32, shape, 0)
    lane = lax.broadcasted_iota(jnp.int32, shape, 1)
    second = lane >= GRID_W
    kc = jnp.where(second, lane - GRID_W, lane)
    col_start = jnp.clip(q - NA_KW // 2, 0, GRID_W - NA_KW)
    col_ok = (kc >= col_start) & (kc < col_start + NA_KW)
    neg = jnp.full(shape, NEG_INF, F32)

    def pair_tile(ri0):
        ring = jnp.broadcast_to(rpb_ref[ri0:ri0 + 1, :], shape)
        tile = pltpu.roll(ring, 0, 1, stride=1, stride_axis=0)
        return jnp.where(col_ok, tile, neg)

    tiles = {}
    for a in range(ATT_ROWS):
        for j in range(0, KEY_ROWS, 2):
            ri0 = j - a + ATT_ROWS - 1
            if ri0 not in tiles:
                tiles[ri0] = pair_tile(ri0)

    windows = (
        lambda a: (ATT_ROWS, ATT_ROWS + NA_KH),
        lambda a: (a, a + NA_KH),
        lambda a: (0, NA_KH),
    )
    for kind, window in enumerate(windows):
        for a in range(ATT_ROWS):
            lo, hi = window(a)
            for j in range(0, KEY_ROWS, 2):
                tile = tiles[j - a + ATT_ROWS - 1]
                ok0, ok1 = lo <= j < hi, lo <= j + 1 < hi
                if ok0 and ok1:
                    val = tile
                elif ok0:
                    val = jnp.where(second, neg, tile)
                elif ok1:
                    val = jnp.where(second, tile, neg)
                else:
                    val = neg
                o_ref[kind, a * GRID_W:(a + 1) * GRID_W, j * GRID_W:(j + 2) * GRID_W] = val


def _bias_rings(rpb):
    heads = rpb.shape[0]
    lo, hi = rpb[:, :N_RI - 1, :], rpb[:, 1:, :]
    gap = jnp.zeros((heads, N_RI - 1, GRID_W - N_CI), rpb.dtype)
    return jnp.concatenate([lo[..., NA_KW - 1:], gap, hi, gap, lo[..., :NA_KW - 1]], axis=-1)


def _bias_tables(rpb):
    heads = rpb.shape[0]
    rings = _bias_rings(rpb)
    assert rings.shape == (heads, N_RI - 1, 2 * GRID_W)
    return pl.pallas_call(
        _bias_table_kernel,
        grid=(heads,),
        in_specs=[pl.BlockSpec((None,) + rings.shape[1:], lambda h: (h, 0, 0))],
        out_specs=pl.BlockSpec((3, None, ATT_TOKENS, ATT_KEYS), lambda h: (0, h, 0, 0)),
        out_shape=jax.ShapeDtypeStruct((3, heads, ATT_TOKENS, ATT_KEYS), F32),
        compiler_params=_params(1),
        name="bias_tables",
    )(rings)


def _even_mix_kernel(x_ref, q_ref, kp_ref, kc_ref, kn_ref, vp_ref, vc_ref, vn_ref, kx_ref, vx_ref,
                     tab_ref, uc_ref, up_ref, un_ref, pw_ref, ps_ref, wo_ref, mod_ref,
                     o_ref, uext_ref, mix_ref, *, seq_len):
    blk = pl.program_id(0)
    n_blk = pl.num_programs(0)
    tq = ATT_TOKENS

    lane = lax.broadcasted_iota(jnp.int32, (1, MXU_DIM), 1)
    for hg in range(NA_HEADS // HEADS_PER_GROUP):
        sl = slice(hg * MXU_DIM, (hg + 1) * MXU_DIM)
        q4 = q_ref[:, sl]
        keys = (kp_ref[:, sl], kc_ref[:, sl], kn_ref[:, sl], kx_ref[:, sl])
        vals = jnp.concatenate([vp_ref[:, sl], vc_ref[:, sl], vn_ref[:, sl], vx_ref[:, sl]], axis=0)
        out4 = jnp.zeros((tq, MXU_DIM), F32)
        for hh in range(HEADS_PER_GROUP):
            head = hg * HEADS_PER_GROUP + hh
            in_head = (lane >= hh * NA_HEAD_DIM) & (lane < (hh + 1) * NA_HEAD_DIM)
            qh = jnp.where(in_head, q4, jnp.zeros_like(q4))
            s = [_dot_nt(qh, kk) for kk in keys]
            for j in range(3):
                s[j] = s[j] + tab_ref[head, :, j * tq:(j + 1) * tq]
            m = jnp.max(jnp.maximum(jnp.maximum(s[0], s[1]), jnp.maximum(s[2], s[3])),
                        axis=1, keepdims=True)
            p = [jnp.exp(sj - m) for sj in s]
            denom = jnp.sum((p[0] + p[1]) + (p[2] + p[3]), axis=1, keepdims=True)
            pv = _dot(jnp.concatenate([pj.astype(BF16) for pj in p], axis=1), vals)
            out4 = jnp.where(in_head, pv * (1.0 / denom), out4)
        mix_ref[:, sl] = out4.astype(BF16)

    halo = POOL_HALO
    uext_ref[0:halo, :] = jnp.where(blk > 0, up_ref[...], 0.0)
    uext_ref[halo:halo + tq, :] = uc_ref[...]
    uext_ref[halo + tq:2 * halo + tq, :] = jnp.where(blk < n_blk - 1, un_ref[...], 0.0)
    t = blk * tq + lax.broadcasted_iota(jnp.int32, (tq, 1), 0)
    n_ext = tq + 2 * halo
    for g, w in enumerate(POOL_WINDOWS):
        cs = slice(g * POOL_GROUP_DIM, (g + 1) * POOL_GROUP_DIM)
        ext = uext_ref[:, cs]
        win = ext + pltpu.roll(ext, 1, 0)
        span = 2
        while span < w:
            win = pltpu.roll(win, span // 2, 0) + pltpu.roll(win, n_ext - span // 2, 0)
            span *= 2
        total = win[halo:halo + tq]
        lo = jnp.clip(t - w // 2, 0, seq_len)
        hi = jnp.clip(t - w // 2 + w, 0, seq_len)
        mean = total * (1.0 / (hi - lo).astype(F32))
        delta = (mean - uc_ref[:, cs]).astype(BF16)
        pooled = _dot(delta, pw_ref[g]) * ps_ref[:, cs]
        mix_ref[:, NA_WIDTH + g * POOL_GROUP_DIM:NA_WIDTH + (g + 1) * POOL_GROUP_DIM] = pooled.astype(BF16)

    y = _dot(mix_ref[...], wo_ref[...])
    o_ref[...] = x_ref[...] + mod_ref[5:6, :] * y


def _even_mix(x, q, k, v, k_ctx, v_ctx, tables, u, pool_w, pool_scale, w_out, e, mod, layer):
    n_tok, d = x.shape
    n_ctx = k_ctx.shape[0]
    pool_width = u.shape[1]
    tq = ATT_TOKENS
    n_blk = n_tok // tq
    halo_per_blk = tq // POOL_HALO
    n_halo = n_tok // POOL_HALO

    cur = lambda width: pl.BlockSpec((tq, width), lambda i: (i, 0))
    prev = pl.BlockSpec((tq, NA_WIDTH), lambda i: (jnp.maximum(i - 1, 0), 0))
    nxt = pl.BlockSpec((tq, NA_WIDTH), lambda i: (jnp.minimum(i + 1, n_blk - 1), 0))
    kind = lambda i: jnp.where(i == 0, 0, jnp.where(i == n_blk - 1, 2, 1))
    return pl.pallas_call(
        functools.partial(_even_mix_kernel, seq_len=n_tok),
        grid=(n_blk,),
        in_specs=[
            cur(d), cur(NA_WIDTH),
            prev, cur(NA_WIDTH), nxt,
            prev, cur(NA_WIDTH), nxt,
            _resident((n_ctx, NA_WIDTH)), _resident((n_ctx, NA_WIDTH)),
            pl.BlockSpec((None, NA_HEADS, tq, ATT_KEYS), lambda i: (kind(i), 0, 0, 0)),
            cur(pool_width),
            pl.BlockSpec((POOL_HALO, pool_width), lambda i: (jnp.maximum(i * halo_per_blk - 1, 0), 0)),
            pl.BlockSpec((POOL_HALO, pool_width), lambda i: (jnp.minimum((i + 1) * halo_per_blk, n_halo - 1), 0)),
            _resident_at(pool_w, e), _resident_at(pool_scale, e), _resident_at(w_out, e),
            _mod_spec(mod, layer, 0),
        ],
        out_specs=cur(d),
        out_shape=jax.ShapeDtypeStruct((n_tok, d), F32),
        scratch_shapes=[
            pltpu.VMEM((tq + 2 * POOL_HALO, pool_width), F32),
            pltpu.VMEM((tq, d), BF16),
        ],
        compiler_params=_params(1),
        name="even_mix",
    )(x, q, k, k, k, v, v, v, k_ctx, v_ctx, tables, u, u, u, pool_w, pool_scale, w_out, mod)


def _conv_mix_kernel(xc_ref, xp_ref, xn_ref, mod_ref, g_ref, win_ref, cw_ref, wo_ref, o_ref, z_ref):
    blk = pl.program_id(0)
    n_blk = pl.num_programs(0)
    tm, d = xc_ref.shape
    halo = SUBLANES
    g, shift, scale = g_ref[...], mod_ref[3:4, :], mod_ref[4:5, :]

    def gate_input(hn):
        cx = _dot(hn, win_ref[:, d:])
        return cx[:, :d] * cx[:, d:]

    hn = _norm_mod(xc_ref[...], g, shift, scale).astype(BF16)
    bg = _dot(hn, win_ref[:, :d])
    z_ref[halo:halo + tm, :] = gate_input(hn)
    x_halo = jnp.concatenate([xp_ref[...], xn_ref[...]], axis=0)
    z_halo = gate_input(_norm_mod(x_halo, g, shift, scale).astype(BF16))
    z_ref[0:halo, :] = jnp.where(blk > 0, z_halo[:halo], 0.0)
    z_ref[halo + tm:, :] = jnp.where(blk < n_blk - 1, z_halo[halo:], 0.0)

    y = (z_ref[halo - 1:halo - 1 + tm, :] * cw_ref[0:1, :]
         + z_ref[halo:halo + tm, :] * cw_ref[1:2, :]
         + z_ref[halo + 1:halo + 1 + tm, :] * cw_ref[2:3, :])
    out = _dot((bg * y).astype(BF16), wo_ref[...])
    o_ref[...] = xc_ref[...] + mod_ref[5:6, :] * out


def _conv_mix(x, mod, layer, gains, w_in, conv_w, w_out, o):
    n_tok, d = x.shape
    tm = min(n_tok, 512)
    n_blk = n_tok // tm
    halo_per_blk = tm // SUBLANES
    n_halo = n_tok // SUBLANES
    return pl.pallas_call(
        _conv_mix_kernel,
        grid=(n_blk,),
        in_specs=[
            pl.BlockSpec((tm, d), lambda i: (i, 0)),
            pl.BlockSpec((SUBLANES, d), lambda i: (jnp.maximum(i * halo_per_blk - 1, 0), 0)),
            pl.BlockSpec((SUBLANES, d), lambda i: (jnp.minimum((i + 1) * halo_per_blk, n_halo - 1), 0)),
            _mod_spec(mod, layer, 0),
            _resident_at(gains, layer, 1), _resident_at(w_in, o), _resident_at(conv_w, o),
            _resident_at(w_out, o),
        ],
        out_specs=pl.BlockSpec((tm, d), lambda i: (i, 0)),
        out_shape=jax.ShapeDtypeStruct((n_tok, d), F32),
        scratch_shapes=[pltpu.VMEM((tm + 2 * SUBLANES, d), F32)],
        compiler_params=_params(1),
        name="conv_mix",
    )(x, x, x, mod, gains, w_in, conv_w, w_out)


def kernel(x, c, ctx, c_ctx, mod_w, mod_b, norm_g, ffn_w13, ffn_w2, even_w_in, even_w_out,
           na_rpb, pool_w, pool_scale, conv_w_in, conv_w, conv_w_out, final_g):
    batch, seq_len, d = x.shape
    depth = mod_w.shape[0]
    assert batch == 1 and c.shape[0] == 1 and ctx.shape[0] == 1
    assert seq_len % ATT_TOKENS == 0 and seq_len // GRID_W >= NA_KH and GRID_W >= NA_KW
    assert even_w_in.shape[2] == 3 * NA_WIDTH + len(POOL_WINDOWS) * POOL_GROUP_DIM

    xs = x.reshape(seq_len, d)
    cs = ctx.reshape(ctx.shape[1], d)
    cond_cols = jnp.stack([c[0], c_ctx], axis=1)
    mod = _adaln(cond_cols, mod_w, mod_b)

    w13 = ffn_w13.astype(BF16)
    w2 = ffn_w2.astype(BF16)
    e_in = even_w_in.astype(BF16)
    e_out = even_w_out.astype(BF16)
    p_w = pool_w.astype(BF16)
    p_scale = pool_scale.reshape(pool_scale.shape[0], 1, -1)
    c_in = conv_w_in.astype(BF16)
    c_out = conv_w_out.astype(BF16)
    gains = norm_g.reshape(depth, 3, 1, d)
    fin = final_g.reshape(1, d)

    for i in range(depth):
        even = i % 2 == 0
        if any(j % 2 == 0 for j in range(i + 1, depth)):
            raise NotImplementedError("context stream carried past layer %d" % i)
        last = i == depth - 1
        xs = _ffn(xs, mod, i, 0, gains, w13, w2, 0)
        if even:
            e = i // 2
            cs = _ffn(cs, mod, i, 1, gains, w13, w2, 0)
            q, k, v, u = _even_in(xs, mod, i, 0, gains, e_in, e)
            _, k_ctx, v_ctx, _ = _even_in(cs, mod, i, 1, gains, e_in, e)
            tables = _bias_tables(na_rpb[e])
            xs = _even_mix(xs, q, k, v, k_ctx, v_ctx, tables, u, p_w, p_scale, e_out, e, mod, i)
        else:
            o = i // 2
            xs = _conv_mix(xs, mod, i, gains, c_in, conv_w, c_out, o)
        xs = _ffn(xs, mod, i, 0, gains, w13, w2, 1, final_g=fin if last else None)
    return xs.reshape(batch, seq_len, d)
```
